```python
import math
import jax, jax.numpy as jnp
from jax import lax
import numpy as np


D_MODEL = 2048
BATCH = 4
SEQ = 4096
DEPTH = 4

GRID_W = 64
CTX_LEN = 256
N_MIXERS = 3
N_RG_LAYERS = (DEPTH + N_MIXERS - 1) // N_MIXERS
N_RW_LAYERS = (DEPTH + N_MIXERS - 2) // N_MIXERS
N_RET_LAYERS = DEPTH // N_MIXERS
DEEPNORM_ALPHA = (2 * DEPTH) ** 0.25
DEEPNORM_BETA = (8 * DEPTH) ** -0.25
LN_EPS = 1e-5

D_RNN = D_MODEL
RG_BLOCKS = 8
RG_BLOCK = D_RNN // RG_BLOCKS
CONV_W = 4
RG_C = 8.0

RW_HEAD = 64
RW_HEADS = D_MODEL // RW_HEAD
RW_DECAY_LORA = 96
RW_ICL_LORA = 96
RW_GATE_LORA = 256
RW_DECAY_SCALE = math.exp(-0.5)
RW_GN_EPS = 64e-5

RET_HEADS = 8
RET_DK = D_MODEL // RET_HEADS
RET_DV = 2 * RET_DK
RET_CHUNK = 128
RET_QK = RET_HEADS * RET_DK
RET_V = RET_HEADS * RET_DV
RET_IN = 2 * RET_QK + 3 * RET_V

N_KEYS = 128
N_EXPERTS = N_KEYS * N_KEYS
PEER_HEADS = 8
PEER_DKEY = 256
PEER_DHALF = PEER_DKEY // 2
PEER_TOPK = 16
PEER_BLOCK = 128

kernel_name = 'hybrid_rglru_rwkv7_retention_peer_dit'


def layer_norm(x, g, b):
    xf = x.astype(jnp.float32)
    mu = jnp.mean(xf, -1, keepdims=True)
    var = jnp.mean(jnp.square(xf - mu), -1, keepdims=True)
    return ((xf - mu) * lax.rsqrt(var + LN_EPS)).astype(x.dtype) * g + b


def head_norm(y, eps):
    yf = y.astype(jnp.float32)
    mu = jnp.mean(yf, -1, keepdims=True)
    var = jnp.mean(jnp.square(yf - mu), -1, keepdims=True)
    return ((yf - mu) * lax.rsqrt(var + eps)).astype(y.dtype)


def l2_normalize(t):
    tf = t.astype(jnp.float32)
    return (tf * lax.rsqrt(jnp.sum(tf * tf, -1, keepdims=True) + 1e-12)).astype(t.dtype)


def adaln_mod(cvec, w, b):
    return jnp.split(jax.nn.silu(cvec) @ w + b, 6, axis=-1)


def modulate(x, shift, scale):
    return x * (1.0 + scale) + shift


def q_shift(x):
    Bn, L, D = x.shape
    rows = L // GRID_W
    g = x.reshape(Bn, rows, GRID_W, D)
    q = D // 4
    left = jnp.pad(g[:, :, :-1, :q], ((0, 0), (0, 0), (1, 0), (0, 0)))
    right = jnp.pad(g[:, :, 1:, q:2 * q], ((0, 0), (0, 0), (0, 1), (0, 0)))
    up = jnp.pad(g[:, :-1, :, 2 * q:3 * q], ((0, 0), (1, 0), (0, 0), (0, 0)))
    down = jnp.pad(g[:, 1:, :, 3 * q:], ((0, 0), (0, 1), (0, 0), (0, 0)))
    return jnp.concatenate([left, right, up, down], -1).reshape(Bn, L, D)


def seq_shift(x):
    h = x.shape[-1] // 2
    prev = jnp.pad(x[:, :-1, :h], ((0, 0), (1, 0), (0, 0)))
    nxt = jnp.pad(x[:, 1:, h:], ((0, 0), (0, 1), (0, 0)))
    return jnp.concatenate([prev, nxt], -1)


def depthwise_conv_centred(x, w, b):
    L = x.shape[1]
    xp = jnp.pad(x, ((0, 0), (CONV_W // 2, CONV_W - 1 - CONV_W // 2), (0, 0)))
    return sum(xp[:, j:j + L] * w[j] for j in range(CONV_W)) + b


def linear_scan(a, b, h0, reverse):
    def op(l, r):
        return (l[0] * r[0], r[0] * l[1] + r[1])
    A, h = lax.associative_scan(op, (a, b), axis=1, reverse=reverse)
    return h if h0 is None else h + A * h0[:, None, :]


def rglru_coeffs(xc, gate_w, gate_b, lam):
    Bn, L, _ = xc.shape
    xb = xc.reshape(Bn, L, RG_BLOCKS, RG_BLOCK)
    pre = jnp.einsum('blnk,gnkj->gblnj', xb, gate_w).reshape(2, Bn, L, D_RNN) + gate_b[:, None, None, :]
    r_gate = jax.nn.sigmoid(pre[0])
    i_gate = jax.nn.sigmoid(pre[1])
    log_a = -RG_C * r_gate * jax.nn.softplus(-lam)
    a = jnp.exp(log_a)
    b = jnp.sqrt(-jnp.expm1(2.0 * log_a)) * (i_gate * xc)
    return a, b


def rglru_mixer(h_ctx, h_lat, w_in, conv_w, conv_b, gate_w, gate_b, lam, w_out, ctx_out):
    u_l = h_lat @ w_in
    xc_l = depthwise_conv_centred(u_l[..., D_RNN:], conv_w, conv_b)
    u_c = h_ctx @ (w_in if ctx_out else w_in[:, D_RNN:])
    xc_c = depthwise_conv_centred(u_c[..., -D_RNN:], conv_w, conv_b)
    rec_c, rec_l = [], []
    for d, rev in enumerate((False, True)):
        a_c, b_c = rglru_coeffs(xc_c, gate_w[d], gate_b[d], lam[d])
        h_c = linear_scan(a_c, b_c, None, rev)
        a_l, b_l = rglru_coeffs(xc_l, gate_w[d], gate_b[d], lam[d])
        rec_l.append(linear_scan(a_l, b_l, h_c[:, 0] if rev else h_c[:, -1], rev))
        rec_c.append(h_c)
    y_l = (jax.nn.gelu(u_l[..., :D_RNN]) * (rec_l[0] + rec_l[1])) @ w_out
    y_c = (jax.nn.gelu(u_c[..., :D_RNN]) * (rec_c[0] + rec_c[1])) @ w_out if ctx_out else None
    return y_c, y_l


def rw_heads(t):
    return t.reshape(t.shape[:-1] + (RW_HEADS, RW_HEAD))


def rwkv7_scan(r, v, kk, w, k, a, S0, reverse):
    def step(S, inp):
        r_t, v_t, kk_t, w_t, k_t, a_t = inp
        S = (S * w_t[:, :, None, :]
             - jnp.einsum('bhvk,bhk->bhv', S, kk_t)[..., None] * (kk_t * a_t)[:, :, None, :]
             + v_t[..., None] * k_t[:, :, None, :])
        return S, jnp.einsum('bhvk,bhk->bhv', S, r_t)
    xs = tuple(jnp.moveaxis(t, 1, 0) for t in (r, v, kk, w, k, a))
    S_end, y = lax.scan(step, S0, xs, reverse=reverse)
    return S_end, jnp.moveaxis(y, 0, 1)


def rwkv7_mixer(h_ctx, h_lat, mu, w_rkv, w_o, dec0, dec1, dec2, icl0, icl1, icl2,
                g1, g2, k_k, k_a, r_k, gn_g, gn_b, ctx_out):
    def project(h, shifted):
        xx = shifted - h
        mix = lambda m: h + xx * mu[m]
        r = mix(0) @ w_rkv[0]
        k = mix(1) @ w_rkv[1]
        v = mix(2) @ w_rkv[2]
        xw, xa = mix(3), mix(4)
        kk = l2_normalize(rw_heads(k * k_k))
        dirs = []
        for d in range(2):
            w = jnp.exp(-RW_DECAY_SCALE * jax.nn.sigmoid(dec0[d] + jnp.tanh(xw @ dec1[d]) @ dec2[d]))
            a = jax.nn.sigmoid(icl0[d] + (xa @ icl1[d]) @ icl2[d])
            kd = k * (1.0 + (a - 1.0) * k_a)
            dirs.append((rw_heads(w), rw_heads(kd), rw_heads(a)))
        return rw_heads(r), rw_heads(v), kk, dirs, h, xx

    def finish(p, ys):
        r, v, kk, dirs, h, xx = p
        Bn, L = r.shape[:2]
        y = head_norm(ys[0] + ys[1], RW_GN_EPS).reshape(Bn, L, D_MODEL) * gn_g + gn_b
        bonus = sum(jnp.sum(r * kd * r_k, -1, keepdims=True) * v for (_, kd, _) in dirs)
        g = jax.nn.sigmoid((h + xx * mu[5]) @ g1) @ g2
        return ((y + bonus.reshape(Bn, L, D_MODEL)) * g) @ w_o

    pc = project(h_ctx, seq_shift(h_ctx))
    pl = project(h_lat, q_shift(h_lat))
    S0 = jnp.zeros((h_lat.shape[0], RW_HEADS, RW_HEAD, RW_HEAD), h_lat.dtype)
    ys_c, ys_l = [], []
    for d, rev in enumerate((False, True)):
        S_ctx, yc = rwkv7_scan(pc[0], pc[1], pc[2], *pc[3][d], S0, rev)
        _, yl = rwkv7_scan(pl[0], pl[1], pl[2], *pl[3][d], S_ctx, rev)
        ys_c.append(yc)
        ys_l.append(yl)
    y_l = finish(pl, ys_l)
    y_c = finish(pc, ys_c) if ctx_out else None
    return y_c, y_l


def rotary(t, pos):
    half = t.shape[-1] // 2
    theta = 1.0 / (10000.0 ** jnp.linspace(0.0, 1.0, half, dtype=jnp.float32))
    ang = pos[:, None] * theta[None, :]
    cos = jnp.cos(ang)[None, :, None, :]
    sin = jnp.sin(ang)[None, :, None, :]
    t1 = t[..., :half].astype(jnp.float32)
    t2 = t[..., half:].astype(jnp.float32)
    return jnp.concatenate([t1 * cos - t2 * sin, t1 * sin + t2 * cos], -1).astype(t.dtype)


def retention_chunkwise(q, k, v, R0, log_gamma):
    Bn, H, L, _ = q.shape
    n_chunks = L // RET_CHUNK
    dt = q.dtype
    pos = jnp.arange(RET_CHUNK, dtype=jnp.float32)
    diff = pos[:, None] - pos[None, :]
    decay_mask = jnp.where(diff >= 0, jnp.exp(jnp.maximum(diff, 0.0) * log_gamma[:, None, None]), 0.0).astype(dt)
    xi = jnp.exp((pos + 1.0) * log_gamma[:, None]).astype(dt)[..., None]
    zeta = jnp.exp((RET_CHUNK - 1.0 - pos) * log_gamma[:, None]).astype(dt)[..., None]
    chunk_decay = jnp.exp(RET_CHUNK * log_gamma).astype(dt)[:, None, None]

    def chunks(t):
        return jnp.moveaxis(t.reshape(Bn, H, n_chunks, RET_CHUNK, t.shape[-1]), 2, 0)

    def step(R, qkv):
        qc, kc, vc = qkv
        scores = jnp.einsum('bhid,bhjd->bhij', qc, kc) * decay_mask
        o = jnp.einsum('bhij,bhje->bhie', scores, vc) + jnp.einsum('bhid,bhde->bhie', qc * xi, R)
        R = R * chunk_decay + jnp.einsum('bhjd,bhje->bhde', kc * zeta, vc)
        return R, o

    R, o = lax.scan(step, R0, (chunks(q), chunks(k), chunks(v)))
    return R, jnp.moveaxis(o, 0, 2).reshape(Bn, H, L, RET_DV)


def retention_mixer(h_ctx, h_lat, w_in, w_out, ctx_out):
    log_gamma = jnp.log1p(-jnp.exp2(-5.0 - jnp.arange(RET_HEADS, dtype=jnp.float32)))

    def split_heads(u, pos0):
        Bn, L, _ = u.shape
        pos = pos0 + jnp.arange(L, dtype=jnp.float32)
        q = rotary(u[..., :RET_QK].reshape(Bn, L, RET_HEADS, RET_DK), pos)
        k = rotary(u[..., RET_QK:2 * RET_QK].reshape(Bn, L, RET_HEADS, RET_DK), pos) * RET_DK ** -0.5
        v = u[..., 2 * RET_QK:2 * RET_QK + RET_V].reshape(Bn, L, RET_HEADS, RET_DV)
        return tuple(jnp.swapaxes(t, 1, 2) for t in (q, k, v))

    u_l = h_lat @ w_in
    u_c = h_ctx @ (w_in if ctx_out else w_in[:, :2 * RET_QK + RET_V])
    qc, kc, vc = split_heads(u_c, 0.0)
    ql, kl, vl = split_heads(u_l, float(CTX_LEN))
    flip = lambda t: t[:, :, ::-1]
    R0 = jnp.zeros((h_lat.shape[0], RET_HEADS, RET_DK, RET_DV), ql.dtype)
    R_f, oc_f = retention_chunkwise(qc, kc, vc, R0, log_gamma)
    _, ol_f = retention_chunkwise(ql, kl, vl, R_f, log_gamma)
    R_b, oc_b = retention_chunkwise(flip(qc), flip(kc), flip(vc), R0, log_gamma)
    _, ol_b = retention_chunkwise(flip(ql), flip(kl), flip(vl), R_b, log_gamma)

    def merge(u, o_f, o_b_rev):
        Bn, L, _ = u.shape
        g_f = u[..., 2 * RET_QK + RET_V:2 * RET_QK + 2 * RET_V]
        g_b = u[..., 2 * RET_QK + 2 * RET_V:]
        n_f = jnp.swapaxes(head_norm(o_f, LN_EPS), 1, 2).reshape(Bn, L, RET_V)
        n_b = jnp.swapaxes(head_norm(flip(o_b_rev), LN_EPS), 1, 2).reshape(Bn, L, RET_V)
        return (jax.nn.silu(g_f) * n_f + jax.nn.silu(g_b) * n_b) @ w_out

    y_l = merge(u_l, ol_f, ol_b)
    y_c = merge(u_c, oc_f, oc_b) if ctx_out else None
    return y_c, y_l


def peer_ffn(tok, w_q, sub_keys, u_tab, v_tab):
    T, D = tok.shape

    def block(xb):
        q = (xb @ w_q).reshape(PEER_BLOCK, PEER_HEADS, 2, PEER_DHALF)
        s = jnp.einsum('thpd,hpkd->thpk', q, sub_keys)
        s_top, i_top = lax.top_k(s, PEER_TOPK)
        cand_s = (s_top[:, :, 0, :, None] + s_top[:, :, 1, None, :]).reshape(PEER_BLOCK, PEER_HEADS, PEER_TOPK * PEER_TOPK)
        cand_e = (i_top[:, :, 0, :, None] * N_KEYS + i_top[:, :, 1, None, :]).reshape(PEER_BLOCK, PEER_HEADS, PEER_TOPK * PEER_TOPK)
        best_s, best_j = lax.top_k(cand_s, PEER_TOPK)
        expert = jnp.take_along_axis(cand_e, best_j, axis=-1)
        g = jax.nn.softmax(best_s.astype(jnp.float32), axis=-1).astype(xb.dtype)
        act = jax.nn.gelu(jnp.einsum('thkd,td->thk', u_tab[expert], xb))
        return jnp.einsum('thk,thkd->td', g * act, v_tab[expert])

    return lax.map(block, tok.reshape(T // PEER_BLOCK, PEER_BLOCK, D)).reshape(T, D)


def setup_inputs(seed: int = 0) -> dict:
    key = jax.random.key(seed)
    ks = iter(jax.random.split(key, 64))
    nrm = lambda shape, std: jax.random.normal(next(ks), shape, jnp.float32) * std
    D = D_MODEL
    a_pow = jax.random.uniform(next(ks), (N_RG_LAYERS, 2, D_RNN), jnp.float32, 0.9, 0.999)
    a_base = a_pow ** (1.0 / RG_C)
    return {
        'x': nrm((BATCH, SEQ, D), 1.0),
        'c': nrm((BATCH, D), 1.0),
        'ctx': nrm((BATCH, CTX_LEN, D), 1.0),
        'c_ctx': nrm((D,), 1.0),
        'ada_w': nrm((DEPTH, D, 6 * D), D ** -0.5),
        'ada_b': nrm((DEPTH, 6 * D), 0.02),
        'ln_g': 1.0 + nrm((DEPTH, 2, D), 0.02),
        'ln_b': nrm((DEPTH, 2, D), 0.02),
        'peer_wq': nrm((DEPTH, D, PEER_HEADS * PEER_DKEY), D ** -0.5),
        'peer_keys': nrm((DEPTH, PEER_HEADS, 2, N_KEYS, PEER_DHALF), PEER_DHALF ** -0.5),
        'peer_u': nrm((DEPTH, N_EXPERTS, D), D ** -0.5),
        'peer_v': nrm((DEPTH, N_EXPERTS, D), DEEPNORM_BETA * PEER_HEADS ** -0.5),
        'rg_w_in': nrm((N_RG_LAYERS, D, 2 * D_RNN), D ** -0.5),
        'rg_conv_w': nrm((N_RG_LAYERS, CONV_W, D_RNN), CONV_W ** -0.5),
        'rg_conv_b': nrm((N_RG_LAYERS, D_RNN), 0.02),
        'rg_gate_w': nrm((N_RG_LAYERS, 2, 2, RG_BLOCKS, RG_BLOCK, RG_BLOCK), RG_BLOCK ** -0.5),
        'rg_gate_b': nrm((N_RG_LAYERS, 2, 2, D_RNN), 0.02),
        'rg_lam': jnp.log(a_base) - jnp.log1p(-a_base),
        'rg_w_out': nrm((N_RG_LAYERS, D_RNN, D), DEEPNORM_BETA * D_RNN ** -0.5),
        'rw_mu': jax.random.uniform(next(ks), (N_RW_LAYERS, 6, D), jnp.float32),
        'rw_w_rkv': nrm((N_RW_LAYERS, 3, D, D), D ** -0.5),
        'rw_w_o': nrm((N_RW_LAYERS, D, D), DEEPNORM_BETA * D ** -0.5),
        'rw_dec0': nrm((N_RW_LAYERS, 2, D), 1.0),
        'rw_dec1': nrm((N_RW_LAYERS, 2, D, RW_DECAY_LORA), D ** -0.5),
        'rw_dec2': nrm((N_RW_LAYERS, 2, RW_DECAY_LORA, D), 0.1 * RW_DECAY_LORA ** -0.5),
        'rw_icl0': nrm((N_RW_LAYERS, 2, D), 0.5),
        'rw_icl1': nrm((N_RW_LAYERS, 2, D, RW_ICL_LORA), D ** -0.5),
        'rw_icl2': nrm((N_RW_LAYERS, 2, RW_ICL_LORA, D), 0.1 * RW_ICL_LORA ** -0.5),
        'rw_g1': nrm((N_RW_LAYERS, D, RW_GATE_LORA), D ** -0.5),
        'rw_g2': nrm((N_RW_LAYERS, RW_GATE_LORA, D), RW_GATE_LORA ** -0.5),
        'rw_k_k': 0.85 + nrm((N_RW_LAYERS, D), 0.02),
        'rw_k_a': 1.0 + nrm((N_RW_LAYERS, D), 0.02),
        'rw_r_k': nrm((N_RW_LAYERS, RW_HEADS, RW_HEAD), 0.1),
        'rw_gn_g': 1.0 + nrm((N_RW_LAYERS, D), 0.02),
        'rw_gn_b': nrm((N_RW_LAYERS, D), 0.02),
        'ret_w_in': nrm((N_RET_LAYERS, D, RET_IN), D ** -0.5),
        'ret_w_out': nrm((N_RET_LAYERS, RET_V, D), DEEPNORM_BETA * RET_V ** -0.5),
    }


def reference(x, c, ctx, c_ctx, ada_w, ada_b, ln_g, ln_b,
              peer_wq, peer_keys, peer_u, peer_v,
              rg_w_in, rg_conv_w, rg_conv_b, rg_gate_w, rg_gate_b, rg_lam, rg_w_out,
              rw_mu, rw_w_rkv, rw_w_o, rw_dec0, rw_dec1, rw_dec2, rw_icl0, rw_icl1, rw_icl2,
              rw_g1, rw_g2, rw_k_k, rw_k_a, rw_r_k, rw_gn_g, rw_gn_b,
              ret_w_in, ret_w_out):
    x_lat, x_ctx = x, ctx
    for i in range(DEPTH):
        kind, j = i % N_MIXERS, i // N_MIXERS
        last = i == DEPTH - 1
        mod_l = adaln_mod(c[:, None, :], ada_w[i], ada_b[i])
        mod_c = adaln_mod(c_ctx, ada_w[i], ada_b[i])
        h_c = modulate(x_ctx, mod_c[0], mod_c[1])
        h_l = modulate(x_lat, mod_l[0], mod_l[1])
        if kind == 0:
            y_c, y_l = rglru_mixer(h_c, h_l, rg_w_in[j], rg_conv_w[j], rg_conv_b[j], rg_gate_w[j],
                                   rg_gate_b[j], rg_lam[j], rg_w_out[j], not last)
        elif kind == 1:
            y_c, y_l = rwkv7_mixer(h_c, h_l, rw_mu[j], rw_w_rkv[j], rw_w_o[j], rw_dec0[j], rw_dec1[j],
                                   rw_dec2[j], rw_icl0[j], rw_icl1[j], rw_icl2[j], rw_g1[j], rw_g2[j],
                                   rw_k_k[j], rw_k_a[j], rw_r_k[j], rw_gn_g[j], rw_gn_b[j], not last)
        else:
            y_c, y_l = retention_mixer(h_c, h_l, ret_w_in[j], ret_w_out[j], not last)
        x_lat = layer_norm(DEEPNORM_ALPHA * x_lat + mod_l[2] * y_l, ln_g[i, 0], ln_b[i, 0])
        h_l = modulate(x_lat, mod_l[3], mod_l[4])
        if last:
            y_l = peer_ffn(h_l.reshape(-1, D_MODEL), peer_wq[i], peer_keys[i], peer_u[i], peer_v[i]).reshape(x_lat.shape)
        else:
            x_ctx = layer_norm(DEEPNORM_ALPHA * x_ctx + mod_c[2] * y_c, ln_g[i, 0], ln_b[i, 0])
            h_c = modulate(x_ctx, mod_c[3], mod_c[4])
            n_ctx_tok = h_c.shape[0] * h_c.shape[1]
            y = peer_ffn(jnp.concatenate([h_c.reshape(-1, D_MODEL), h_l.reshape(-1, D_MODEL)], 0),
                         peer_wq[i], peer_keys[i], peer_u[i], peer_v[i])
            y_c = y[:n_ctx_tok].reshape(x_ctx.shape)
            y_l = y[n_ctx_tok:].reshape(x_lat.shape)
            x_ctx = layer_norm(DEEPNORM_ALPHA * x_ctx + mod_c[5] * y_c, ln_g[i, 1], ln_b[i, 1])
        x_lat = layer_norm(DEEPNORM_ALPHA * x_lat + mod_l[5] * y_l, ln_g[i, 1], ln_b[i, 1])
    return x_lat
```

```python
import functools
import math

import jax
import jax.numpy as jnp
from jax import lax
from jax.experimental import pallas as pl
from jax.experimental.pallas import tpu as pltpu

F32 = jnp.float32
BF16 = jnp.bfloat16

VMEM_LIMIT_BYTES = 56 * 1024 * 1024
LANES = 128
SUBLANES = 8

N_MIXERS = 3
LN_EPS = 1e-5
GRID_W = 64
CONV_W = 4
RG_C = 8.0
RG_BLOCKS = 8
RW_HEAD = 64
RW_DECAY_SCALE = math.exp(-0.5)
RW_GN_EPS = 64e-5
RET_HEADS = 8
RET_CHUNK = 128
N_KEYS = 128
PEER_HEADS = 8
PEER_TOPK = 16
ROW_TILE = 256


def _pick(n, candidates):
    for t in candidates:
        if n % t == 0:
            return t
    return n


def _params(*sem):
    return pltpu.CompilerParams(dimension_semantics=sem, vmem_limit_bytes=VMEM_LIMIT_BYTES)


def _mm_kernel(a_ref, w_ref, o_ref, *, act):
    acc = jnp.dot(a_ref[...], w_ref[...], preferred_element_type=F32)
    if act == "tanh":
        acc = jnp.tanh(acc)
    elif act == "sigmoid":
        acc = jax.nn.sigmoid(acc)
    o_ref[...] = acc.astype(o_ref.dtype)


def matmul(a, w, out_dtype=F32, act=None):
    M, K = a.shape
    N = w.shape[1]
    tm = _pick(M, (512, 256, 128))
    tn = _pick(N, (512, 256, 128))
    return pl.pallas_call(
        functools.partial(_mm_kernel, act=act),
        grid=(M // tm, N // tn),
        in_specs=[pl.BlockSpec((tm, K), lambda i, j: (i, 0)),
                  pl.BlockSpec((K, tn), lambda i, j: (0, j))],
        out_specs=pl.BlockSpec((tm, tn), lambda i, j: (i, j)),
        out_shape=jax.ShapeDtypeStruct((M, N), out_dtype),
        compiler_params=_params("parallel", "parallel"),
    )(a, w)


def _ln_mod_kernel(x_ref, y_ref, mg_ref, mm_ref, g_ref, b_ref, xo_ref, ho_ref, *, alpha, gate, shift, scale):
    z = alpha * x_ref[...] + mg_ref[gate:gate + 1, :] * y_ref[...]
    mu = jnp.mean(z, axis=-1, keepdims=True)
    zc = z - mu
    var = jnp.mean(zc * zc, axis=-1, keepdims=True)
    xn = zc * lax.rsqrt(var + LN_EPS) * g_ref[...] + b_ref[...]
    xo_ref[...] = xn
    ho_ref[...] = (xn * (1.0 + mm_ref[scale:scale + 1, :]) + mm_ref[shift:shift + 1, :]).astype(ho_ref.dtype)


def ln_mod(x, y, mod_gate, gate, mod_next, shift, scale, g, b, alpha, ctx_tiles, tiles_per_row, h_dtype):
    T, D = x.shape

    def mod_map(i):
        return (i // tiles_per_row, jnp.where((i % tiles_per_row) >= ctx_tiles, 1, 0), 0, 0)

    return pl.pallas_call(
        functools.partial(_ln_mod_kernel, alpha=alpha, gate=gate, shift=shift, scale=scale),
        grid=(T // ROW_TILE,),
        in_specs=[pl.BlockSpec((ROW_TILE, D), lambda i: (i, 0)),
                  pl.BlockSpec((ROW_TILE, D), lambda i: (i, 0)),
                  pl.BlockSpec((None, None, 6, D), mod_map),
                  pl.BlockSpec((None, None, 6, D), mod_map),
                  pl.BlockSpec((1, D), lambda i: (0, 0)),
                  pl.BlockSpec((1, D), lambda i: (0, 0))],
        out_specs=[pl.BlockSpec((ROW_TILE, D), lambda i: (i, 0)),
                   pl.BlockSpec((ROW_TILE, D), lambda i: (i, 0))],
        out_shape=[jax.ShapeDtypeStruct((T, D), F32), jax.ShapeDtypeStruct((T, D), h_dtype)],
        compiler_params=_params("parallel"),
    )(x, y, mod_gate, mod_next, g.reshape(1, D), b.reshape(1, D))


def _scan_chunk(d, c, ctx_chunks, n_chunks):
    rev = jnp.where(c < ctx_chunks, ctx_chunks - 1 - c, n_chunks - 1 - (c - ctx_chunks))
    return jnp.where(d == 0, c, rev)


def _softplus(x):
    return jnp.maximum(x, 0.0) + jnp.log1p(jnp.exp(-jnp.abs(x)))


def _rg_coef_kernel(xc_ref, gw_ref, gb_ref, lam_ref, a_ref, b_ref):
    xc = xc_ref[...]
    xcb = xc.astype(BF16)
    bs = xc.shape[1] // RG_BLOCKS
    for d in range(2):
        for n in range(RG_BLOCKS):
            sl = slice(n * bs, (n + 1) * bs)
            xs = xcb[:, sl]
            r_gate = jax.nn.sigmoid(jnp.dot(xs, gw_ref[d, 0, n], preferred_element_type=F32) + gb_ref[d, 0:1, sl])
            i_gate = jax.nn.sigmoid(jnp.dot(xs, gw_ref[d, 1, n], preferred_element_type=F32) + gb_ref[d, 1:2, sl])
            log_a = (-RG_C) * r_gate * _softplus(-lam_ref[d:d + 1, sl])
            a = jnp.exp(log_a)
            a_ref[d, :, sl] = a
            b_ref[d, :, sl] = jnp.sqrt(-jnp.tanh(log_a) * (a * a + 1.0)) * (i_gate * xc[:, sl])


def rg_coef(xc, gate_w, gate_b, lam):
    T, C = xc.shape
    bs = C // RG_BLOCKS
    return pl.pallas_call(
        _rg_coef_kernel,
        grid=(T // ROW_TILE,),
        in_specs=[pl.BlockSpec((ROW_TILE, C), lambda i: (i, 0)),
                  pl.BlockSpec((2, 2, RG_BLOCKS, bs, bs), lambda i: (0, 0, 0, 0, 0)),
                  pl.BlockSpec((2, 2, C), lambda i: (0, 0, 0)),
                  pl.BlockSpec((2, C), lambda i: (0, 0))],
        out_specs=[pl.BlockSpec((2, ROW_TILE, C), lambda i: (0, i, 0)),
                   pl.BlockSpec((2, ROW_TILE, C), lambda i: (0, i, 0))],
        out_shape=[jax.ShapeDtypeStruct((2, T, C), F32), jax.ShapeDtypeStruct((2, T, C), F32)],
        compiler_params=_params("parallel"),
    )(xc, gate_w.astype(BF16), gate_b, lam)


def _lin_scan_kernel(a_ref, b_ref, o_ref, h_ref, *, tc):
    d = pl.program_id(0)

    @pl.when(pl.program_id(2) == 0)
    def _():
        h_ref[...] = jnp.zeros_like(h_ref)

    def run(rev):
        def body(i, h):
            t = (tc - 1 - i) if rev else i
            h = a_ref[t] * h + b_ref[t]
            o_ref[t] = h
            return h
        h_ref[...] = lax.fori_loop(0, tc, body, h_ref[...], unroll=8)

    @pl.when(d == 0)
    def _():
        run(False)

    @pl.when(d == 1)
    def _():
        run(True)


def lin_scan(a, b, ctx_len):
    _, B, S, C = a.shape
    tc = ROW_TILE
    cs = C // LANES
    a5 = a.reshape(2, B, S, cs, LANES)
    b5 = b.reshape(2, B, S, cs, LANES)
    n_chunks, ctx_chunks = S // tc, ctx_len // tc
    imap = lambda d, bb, c: (d, bb, _scan_chunk(d, c, ctx_chunks, n_chunks), 0, 0)
    spec = pl.BlockSpec((None, None, tc, cs, LANES), imap)
    out = pl.pallas_call(
        functools.partial(_lin_scan_kernel, tc=tc),
        grid=(2, B, n_chunks),
        in_specs=[spec, spec],
        out_specs=spec,
        out_shape=jax.ShapeDtypeStruct(a5.shape, F32),
        scratch_shapes=[pltpu.VMEM((cs, LANES), F32)],
        compiler_params=_params("arbitrary", "arbitrary", "arbitrary"),
    )(a5, b5)
    return out.reshape(2, B, S, C)


def _rwkv_scan_kernel(r_ref, kk_ref, w_ref, ka_ref, kd_ref, v_ref, y_ref, s_ref, *, tc):
    nk = s_ref.shape[0]

    @pl.when(pl.program_id(1) == 0)
    def _():
        s_ref[...] = jnp.zeros_like(s_ref)

    def step(t, carry):
        parts = [None] * 4
        for k in range(nk):
            term = s_ref[k] * kk_ref[t, k:k + 1, :]
            parts[k % 4] = term if parts[k % 4] is None else parts[k % 4] + term
        sa = (parts[0] + parts[1]) + (parts[2] + parts[3])
        v = v_ref[t]
        yp = [None] * 4
        for k in range(nk):
            sk = s_ref[k] * w_ref[t, k:k + 1, :] - sa * ka_ref[t, k:k + 1, :] + v * kd_ref[t, k:k + 1, :]
            s_ref[k] = sk
            term = sk * r_ref[t, k:k + 1, :]
            yp[k % 4] = term if yp[k % 4] is None else yp[k % 4] + term
        y_ref[t] = (yp[0] + yp[1]) + (yp[2] + yp[3])
        return carry

    lax.fori_loop(0, tc, step, 0)


def rwkv_scan(r, kk, w, ka, kd, v):
    _, S, N, BH = r.shape
    tc = 32
    spec = pl.BlockSpec((None, tc, N, BH), lambda d, c: (d, c, 0, 0))
    return pl.pallas_call(
        functools.partial(_rwkv_scan_kernel, tc=tc),
        grid=(2, S // tc),
        in_specs=[spec] * 6,
        out_specs=spec,
        out_shape=jax.ShapeDtypeStruct(r.shape, F32),
        scratch_shapes=[pltpu.VMEM((N, N, BH), F32)],
        compiler_params=_params("arbitrary", "arbitrary"),
    )(r, kk, w, ka, kd, v)


def _ret_kernel(q_ref, k_ref, v_ref, mask_ref, xi_ref, zeta_ref, cd_ref, o_ref, r_ref):
    @pl.when(pl.program_id(3) == 0)
    def _():
        r_ref[...] = jnp.zeros_like(r_ref)

    q = q_ref[...]
    k = k_ref[...]
    vb = v_ref[...].astype(BF16)
    s = lax.dot_general(q.astype(BF16), k.astype(BF16), (((1,), (1,)), ((), ())),
                        preferred_element_type=F32) * mask_ref[...]
    o = jnp.dot(s.astype(BF16), vb, preferred_element_type=F32)
    o = o + jnp.dot((q * xi_ref[...]).astype(BF16), r_ref[...].astype(BF16), preferred_element_type=F32)
    kz = (k * zeta_ref[...]).astype(BF16)
    r_ref[...] = r_ref[...] * cd_ref[...] + lax.dot_general(kz, vb, (((0,), (0,)), ((), ())),
                                                             preferred_element_type=F32)
    o_ref[...] = o


def retention(q, k, v, ctx_len):
    B, S, _ = q.shape
    H = RET_HEADS
    dk, dv = q.shape[2] // H, v.shape[2] // H
    C = RET_CHUNK
    log_gamma = jnp.log1p(-jnp.exp2(-5.0 - jnp.arange(H, dtype=F32)))[:, None, None]
    pos = jnp.arange(C, dtype=F32)
    diff = pos[:, None] - pos[None, :]
    mask_f = jnp.where(diff >= 0, jnp.exp(jnp.maximum(diff, 0.0) * log_gamma), 0.0)
    mask = jnp.stack([mask_f, jnp.swapaxes(mask_f, 1, 2)], axis=1)
    xi_f = jnp.exp((pos + 1.0) * log_gamma[:, 0])
    xi = jnp.stack([xi_f, xi_f[:, ::-1]], axis=1)[..., None] * jnp.ones((dk,), F32)
    zeta_f = jnp.exp((C - 1.0 - pos) * log_gamma[:, 0])
    zeta = jnp.stack([zeta_f, zeta_f[:, ::-1]], axis=1)[..., None] * jnp.ones((dk,), F32)
    cd = jnp.exp(C * log_gamma) * jnp.ones((1, dv), F32)
    n_chunks, ctx_chunks = S // C, ctx_len // C
    chunk = lambda d, c: _scan_chunk(d, c, ctx_chunks, n_chunks)
    return pl.pallas_call(
        _ret_kernel,
        grid=(B, H, 2, n_chunks),
        in_specs=[pl.BlockSpec((None, C, dk), lambda b, h, d, c: (b, chunk(d, c), h)),
                  pl.BlockSpec((None, C, dk), lambda b, h, d, c: (b, chunk(d, c), h)),
                  pl.BlockSpec((None, C, dv), lambda b, h, d, c: (b, chunk(d, c), h)),
                  pl.BlockSpec((None, None, C, C), lambda b, h, d, c: (h, d, 0, 0)),
                  pl.BlockSpec((None, None, C, dk), lambda b, h, d, c: (h, d, 0, 0)),
                  pl.BlockSpec((None, None, C, dk), lambda b, h, d, c: (h, d, 0, 0)),
                  pl.BlockSpec((None, 1, dv), lambda b, h, d, c: (h, 0, 0))],
        out_specs=pl.BlockSpec((None, None, C, dv), lambda b, h, d, c: (d, b, chunk(d, c), h)),
        out_shape=jax.ShapeDtypeStruct((2, B, S, H * dv), F32),
        scratch_shapes=[pltpu.VMEM((dk, dv), F32)],
        compiler_params=_params("arbitrary", "arbitrary", "arbitrary", "arbitrary"),
    )(q, k, v, mask, xi, zeta, cd)


def _cmpx(v, i, j):
    hi, lo = jnp.maximum(v[i], v[j]), jnp.minimum(v[i], v[j])
    v[i], v[j] = hi, lo


def _bitonic_sort_desc(v):
    n = len(v)
    k = 2
    while k <= n:
        j = k // 2
        while j >= 1:
            for i in range(n):
                l = i ^ j
                if l > i:
                    if (i & k) == 0:
                        _cmpx(v, i, l)
                    else:
                        _cmpx(v, l, i)
            j //= 2
        k *= 2
    return v


def _bitonic_merge_desc(v):
    j = len(v) // 2
    while j >= 1:
        for i in range(len(v)):
            l = i ^ j
            if l > i:
                _cmpx(v, i, l)
        j //= 2
    return v


def _merge_top(a, b):
    n = len(a)
    return _bitonic_merge_desc([jnp.maximum(a[r], b[n - 1 - r]) for r in range(n)])


def _top16_rows(s):
    groups = [s[SUBLANES * r:SUBLANES * (r + 1), :] for r in range(s.shape[0] // SUBLANES)]
    top = _bitonic_sort_desc(groups)
    for sh in (4, 2, 1):
        top = _merge_top(top, [pltpu.roll(t, sh, 0) for t in top])
    return top


def _peer_score_kernel(q_ref, keys_ref, s1_ref, c_ref, s2_ref, e2_ref, tau_ref):
    tm = q_ref.shape[0]
    dh = keys_ref.shape[3]
    sub = lax.broadcasted_iota(jnp.int32, (SUBLANES, tm), 0)
    a_pack = [jnp.zeros((SUBLANES, tm), F32) for _ in range(PEER_TOPK)]
    b_pack = [jnp.zeros((SUBLANES, tm), F32) for _ in range(PEER_TOPK)]
    for h in range(PEER_HEADS):
        for p in range(2):
            col = (2 * h + p) * dh
            s = lax.dot_general(keys_ref[h, p], q_ref[:, col:col + dh], (((1,), (1,)), ((), ())),
                                preferred_element_type=F32)
            (s1_ref if p == 0 else s2_ref)[h] = s
            top = _top16_rows(s)
            pack = a_pack if p == 0 else b_pack
            for r in range(PEER_TOPK):
                pack[r] = jnp.where(sub == h, top[r], pack[r])
    first = [a_pack[0] + b_pack[j] for j in range(PEER_TOPK)]
    rest = [a_pack[i] + b_pack[j] for i in range(1, PEER_TOPK) for j in range(PEER_TOPK)
            if (i + 1) * (j + 1) <= PEER_TOPK]
    neg = jnp.full((SUBLANES, tm), -jnp.inf, F32)
    rest = rest + [neg] * (-len(rest) % PEER_TOPK)
    best = first
    for g in range(len(rest) // PEER_TOPK):
        best = _merge_top(best, _bitonic_sort_desc(rest[g * PEER_TOPK:(g + 1) * PEER_TOPK]))
    tau = best[PEER_TOPK - 1]
    m = best[0]
    z = jnp.exp(best[0] - m)
    for r in range(1, PEER_TOPK):
        z = z + jnp.exp(best[r] - m)
    tau_ref[...] = tau
    inv_z = 1.0 / z
    for h in range(PEER_HEADS):
        c_ref[h] = jnp.exp(s1_ref[h] - a_pack[0][h:h + 1, :]) * inv_z[h:h + 1, :]
        e2_ref[h] = jnp.exp(s2_ref[h] - b_pack[0][h:h + 1, :])


def peer_scores(q, keys):
    T = q.shape[0]
    H, _, nk, dh = keys.shape
    tm = ROW_TILE
    tab = jax.ShapeDtypeStruct((H, nk, T), F32)
    tab_spec = pl.BlockSpec((H, nk, tm), lambda i: (0, 0, i))
    return pl.pallas_call(
        _peer_score_kernel,
        grid=(T // tm,),
        in_specs=[pl.BlockSpec((tm, q.shape[1]), lambda i: (i, 0)),
                  pl.BlockSpec((H, 2, nk, dh), lambda i: (0, 0, 0, 0))],
        out_specs=[tab_spec, tab_spec, tab_spec, tab_spec, pl.BlockSpec((H, tm), lambda i: (0, i))],
        out_shape=[tab, tab, tab, tab, jax.ShapeDtypeStruct((H, T), F32)],
        compiler_params=_params("parallel"),
    )(q, keys)


def _gelu_tanh(x):
    return 0.5 * x * (1.0 + jnp.tanh(0.7978845608028654 * (x + 0.044715 * (x * x * x))))


def _peer_dense_kernel(ht_ref, u_ref, vt_ref, s1_ref, c_ref, s2_ref, e2_ref, tau_ref, o_ref, *, te):
    et = pl.program_id(1)
    tm = ht_ref.shape[1]

    @pl.when(et == 0)
    def _():
        o_ref[...] = jnp.zeros_like(o_ref)

    act = _gelu_tanh(jnp.dot(u_ref[...], ht_ref[...], preferred_element_type=F32))
    rows = te // N_KEYS
    group = pl.multiple_of((et * rows) // SUBLANES * SUBLANES, SUBLANES)
    offset = (et * rows) % SUBLANES

    def table_row(ref, h, ls, ii):
        blk = ref[h, pl.ds(group, SUBLANES), ls]
        row = blk[ii:ii + 1, :]
        for o in range(rows, SUBLANES, rows):
            row = jnp.where(offset == o, blk[o + ii:o + ii + 1, :], row)
        return row

    parts = []
    for ii in range(rows):
        cols = []
        for lb in range(tm // LANES):
            ls = slice(lb * LANES, (lb + 1) * LANES)
            w = jnp.zeros((N_KEYS, LANES), F32)
            for h in range(PEER_HEADS):
                ssum = s2_ref[h, :, ls] + table_row(s1_ref, h, ls, ii)
                w = w + jnp.where(ssum >= tau_ref[h:h + 1, ls], e2_ref[h, :, ls] * table_row(c_ref, h, ls, ii), 0.0)
            cols.append((w * act[ii * N_KEYS:(ii + 1) * N_KEYS, ls]).astype(BF16))
        parts.append(jnp.concatenate(cols, axis=1))
    pt = jnp.concatenate(parts, axis=0)
    o_ref[...] += jnp.dot(vt_ref[...], pt, preferred_element_type=F32)


def peer_dense(ht, u, vt, s1, c, s2, e2, tau):
    D, T = ht.shape
    E = u.shape[0]
    H = s1.shape[0]
    tm = _pick(T, (512, 256))
    te = 512
    tab_spec = pl.BlockSpec((H, N_KEYS, tm), lambda i, e: (0, 0, i))
    return pl.pallas_call(
        functools.partial(_peer_dense_kernel, te=te),
        grid=(T // tm, E // te),
        in_specs=[pl.BlockSpec((D, tm), lambda i, e: (0, i)),
                  pl.BlockSpec((te, D), lambda i, e: (e, 0)),
                  pl.BlockSpec((D, te), lambda i, e: (0, e)),
                  tab_spec, tab_spec, tab_spec, tab_spec,
                  pl.BlockSpec((H, tm), lambda i, e: (0, i))],
        out_specs=pl.BlockSpec((D, tm), lambda i, e: (0, i)),
        out_shape=jax.ShapeDtypeStruct((D, T), F32),
        compiler_params=_params("parallel", "arbitrary"),
    )(ht, u, vt, s1, c, s2, e2, tau)


def peer_ffn(h, w_q, keys, u_tab, v_tab):
    q = matmul(h, w_q.astype(BF16), out_dtype=BF16)
    s1, c, s2, e2, tau = peer_scores(q, keys.astype(BF16))
    yt = peer_dense(h.T, u_tab.astype(BF16), v_tab.T.astype(BF16), s1, c, s2, e2, tau)
    return yt.T


def _head_norm(y, eps):
    mu = jnp.mean(y, -1, keepdims=True)
    var = jnp.mean(jnp.square(y - mu), -1, keepdims=True)
    return (y - mu) * lax.rsqrt(var + eps)


def _conv_centred(x, w, b):
    L = x.shape[1]
    xp = jnp.pad(x, ((0, 0), (CONV_W // 2, CONV_W - 1 - CONV_W // 2), (0, 0)))
    return sum(xp[:, j:j + L] * w[j] for j in range(CONV_W)) + b


def rglru_mixer(h, ctx_len, w_in, conv_w, conv_b, gate_w, gate_b, lam, w_out):
    B, S, D = h.shape
    T = B * S
    C = w_in.shape[1] // 2
    u = matmul(h.reshape(T, D), w_in.astype(BF16))
    ur = u[:, C:].reshape(B, S, C)
    xc = jnp.concatenate([_conv_centred(ur[:, :ctx_len], conv_w, conv_b),
                          _conv_centred(ur[:, ctx_len:], conv_w, conv_b)], axis=1)
    a, b = rg_coef(xc.reshape(T, C), gate_w, gate_b, lam)
    rec = lin_scan(a.reshape(2, B, S, C), b.reshape(2, B, S, C), ctx_len)
    z = jax.nn.gelu(u[:, :C]) * (rec[0] + rec[1]).reshape(T, C)
    return matmul(z.astype(BF16), w_out.astype(BF16))


def _q_shift(x):
    Bn, L, D = x.shape
    g = x.reshape(Bn, L // GRID_W, GRID_W, D)
    q = D // 4
    left = jnp.pad(g[:, :, :-1, :q], ((0, 0), (0, 0), (1, 0), (0, 0)))
    right = jnp.pad(g[:, :, 1:, q:2 * q], ((0, 0), (0, 0), (0, 1), (0, 0)))
    up = jnp.pad(g[:, :-1, :, 2 * q:3 * q], ((0, 0), (1, 0), (0, 0), (0, 0)))
    down = jnp.pad(g[:, 1:, :, 3 * q:], ((0, 0), (0, 1), (0, 0), (0, 0)))
    return jnp.concatenate([left, right, up, down], -1).reshape(Bn, L, D)


def _seq_shift(x):
    hh = x.shape[-1] // 2
    prev = jnp.pad(x[:, :-1, :hh], ((0, 0), (1, 0), (0, 0)))
    nxt = jnp.pad(x[:, 1:, hh:], ((0, 0), (0, 1), (0, 0)))
    return jnp.concatenate([prev, nxt], -1)


def _pad_cols(w):
    return jnp.pad(w, ((0, 0), (0, -w.shape[1] % LANES)))


def _pad_rows(w):
    return jnp.pad(w, ((0, -w.shape[0] % LANES), (0, 0)))


def rwkv7_mixer(h, ctx_len, mu, w_rkv, w_o, dec0, dec1, dec2, icl0, icl1, icl2, g1, g2, k_k, k_a, r_k, gn_g, gn_b):
    B, S, D = h.shape
    T = B * S
    H, N = D // RW_HEAD, RW_HEAD
    shifted = jnp.concatenate([_seq_shift(h[:, :ctx_len]), _q_shift(h[:, ctx_len:])], axis=1)
    xx = shifted - h
    mix = lambda m: (h + xx * mu[m]).reshape(T, D).astype(BF16)
    r = matmul(mix(0), w_rkv[0].astype(BF16))
    k = matmul(mix(1), w_rkv[1].astype(BF16))
    v = matmul(mix(2), w_rkv[2].astype(BF16))
    xw, xa = mix(3), mix(4)
    kn = (k * k_k).reshape(T, H, N)
    kk = (kn * lax.rsqrt(jnp.sum(kn * kn, -1, keepdims=True) + 1e-12)).reshape(T, D)

    def reorder(t, d):
        t = t.reshape(B, S, H, N)
        if d == 1:
            t = jnp.concatenate([t[:, :ctx_len][:, ::-1], t[:, ctx_len:][:, ::-1]], axis=1)
        return t.transpose(1, 3, 0, 2).reshape(S, N, B * H)

    def restore(t, d):
        t = t.reshape(S, N, B, H).transpose(2, 0, 3, 1)
        if d == 1:
            t = jnp.concatenate([t[:, :ctx_len][:, ::-1], t[:, ctx_len:][:, ::-1]], axis=1)
        return t.reshape(T, D)

    ws, kas, kds, bonus = [], [], [], 0.0
    for d in range(2):
        lw = matmul(matmul(xw, _pad_cols(dec1[d]).astype(BF16), out_dtype=BF16, act="tanh"),
                    _pad_rows(dec2[d]).astype(BF16))
        w = jnp.exp(-RW_DECAY_SCALE * jax.nn.sigmoid(dec0[d] + lw))
        la = matmul(matmul(xa, _pad_cols(icl1[d]).astype(BF16), out_dtype=BF16), _pad_rows(icl2[d]).astype(BF16))
        a = jax.nn.sigmoid(icl0[d] + la)
        kd = k * (1.0 + (a - 1.0) * k_a)
        ws.append(reorder(w, d))
        kas.append(reorder(kk * a, d))
        kds.append(reorder(kd, d))
        bonus = bonus + jnp.sum((r * kd).reshape(T, H, N) * r_k, -1, keepdims=True) * v.reshape(T, H, N)
    stack = lambda f: jnp.stack([f(0), f(1)], axis=0)
    ys = rwkv_scan(stack(lambda d: reorder(r, d)), stack(lambda d: reorder(kk, d)), jnp.stack(ws), jnp.stack(kas),
                   jnp.stack(kds), stack(lambda d: reorder(v, d)))
    ysum = (restore(ys[0], 0) + restore(ys[1], 1)).reshape(T, H, N)
    y = _head_norm(ysum, RW_GN_EPS).reshape(T, D) * gn_g + gn_b
    g = matmul(matmul(mix(5), g1.astype(BF16), out_dtype=BF16, act="sigmoid"), g2.astype(BF16))
    return matmul(((y + bonus.reshape(T, D)) * g).astype(BF16), w_o.astype(BF16))


def _rotary(t, pos):
    half = t.shape[-1] // 2
    theta = 1.0 / (10000.0 ** jnp.linspace(0.0, 1.0, half, dtype=F32))
    ang = pos[:, None] * theta[None, :]
    cos = jnp.cos(ang)[None, :, None, :]
    sin = jnp.sin(ang)[None, :, None, :]
    t1, t2 = t[..., :half], t[..., half:]
    return jnp.concatenate([t1 * cos - t2 * sin, t1 * sin + t2 * cos], -1)


def retention_mixer(h, ctx_len, w_in, w_out):
    B, S, D = h.shape
    T = B * S
    H = RET_HEADS
    qk = D
    dv = 2 * (D // H)
    nv = H * dv
    u = matmul(h.reshape(T, D), w_in.astype(BF16)).reshape(B, S, -1)
    pos = jnp.arange(S, dtype=F32)
    q = _rotary(u[..., :qk].reshape(B, S, H, -1), pos).reshape(B, S, qk)
    k = (_rotary(u[..., qk:2 * qk].reshape(B, S, H, -1), pos) * (qk // H) ** -0.5).reshape(B, S, qk)
    v = u[..., 2 * qk:2 * qk + nv]
    o = retention(q, k, v, ctx_len)
    g_f = u[..., 2 * qk + nv:2 * qk + 2 * nv]
    g_b = u[..., 2 * qk + 2 * nv:]
    n_f = _head_norm(o[0].reshape(B, S, H, dv), LN_EPS).reshape(B, S, nv)
    n_b = _head_norm(o[1].reshape(B, S, H, dv), LN_EPS).reshape(B, S, nv)
    z = jax.nn.silu(g_f) * n_f + jax.nn.silu(g_b) * n_b
    return matmul(z.reshape(T, nv).astype(BF16), w_out.astype(BF16))


def kernel(x, c, ctx, c_ctx, ada_w, ada_b, ln_g, ln_b, peer_wq, peer_keys, peer_u, peer_v, rg_w_in, rg_conv_w, rg_conv_b, rg_gate_w, rg_gate_b, rg_lam, rg_w_out, rw_mu, rw_w_rkv, rw_w_o, rw_dec0, rw_dec1, rw_dec2, rw_icl0, rw_icl1, rw_icl2, rw_g1, rw_g2, rw_k_k, rw_k_a, rw_r_k, rw_gn_g, rw_gn_b, ret_w_in, ret_w_out):
    B, L, D = x.shape
    ctx_len = ctx.shape[1]
    S = ctx_len + L
    T = B * S
    depth = ada_w.shape[0]
    alpha = (2 * depth) ** 0.25
    assert ctx_len % ROW_TILE == 0 and L % ROW_TILE == 0

    cond = jnp.zeros((16, D), F32).at[:B].set(c).at[B].set(c_ctx)
    sc = jax.nn.silu(cond).astype(BF16)
    mods = []
    for i in range(depth):
        m = (matmul(sc, ada_w[i].astype(BF16)) + ada_b[i]).reshape(16, 6, D)
        mods.append(jnp.stack([jnp.broadcast_to(m[B], (B, 6, D)), m[:B]], axis=1))

    tiles_per_row, ctx_tiles = S // ROW_TILE, ctx_len // ROW_TILE
    is_lat = (jnp.arange(S) >= ctx_len)[None, :, None]
    xs = jnp.concatenate([ctx, x], axis=1)
    rowmod = lambda m, j: jnp.where(is_lat, m[:, 1, j][:, None, :], m[:, 0, j][:, None, :])
    mixer_dtype = lambda i: F32 if i % N_MIXERS == 1 else BF16
    h = (xs * (1.0 + rowmod(mods[0], 1)) + rowmod(mods[0], 0)).astype(mixer_dtype(0))
    xs = xs.reshape(T, D)

    for i in range(depth):
        kind, j = i % N_MIXERS, i // N_MIXERS
        if kind == 0:
            y = rglru_mixer(h, ctx_len, rg_w_in[j], rg_conv_w[j], rg_conv_b[j], rg_gate_w[j], rg_gate_b[j],
                            rg_lam[j], rg_w_out[j])
        elif kind == 1:
            y = rwkv7_mixer(h, ctx_len, rw_mu[j], rw_w_rkv[j], rw_w_o[j], rw_dec0[j], rw_dec1[j], rw_dec2[j],
                            rw_icl0[j], rw_icl1[j], rw_icl2[j], rw_g1[j], rw_g2[j], rw_k_k[j], rw_k_a[j],
                            rw_r_k[j], rw_gn_g[j], rw_gn_b[j])
        else:
            y = retention_mixer(h, ctx_len, ret_w_in[j], ret_w_out[j])
        xs, h2 = ln_mod(xs, y, mods[i], 2, mods[i], 3, 4, ln_g[i, 0], ln_b[i, 0], alpha, ctx_tiles,
                        tiles_per_row, BF16)
        y2 = peer_ffn(h2, peer_wq[i], peer_keys[i], peer_u[i], peer_v[i])
        nxt = mods[min(i + 1, depth - 1)]
        xs, h = ln_mod(xs, y2, mods[i], 5, nxt, 0, 1, ln_g[i, 1], ln_b[i, 1], alpha, ctx_tiles,
                       tiles_per_row, mixer_dtype(i + 1))
        h = h.reshape(B, S, D)
    return xs.reshape(B, S, D)[:, ctx_len:]
```

```python
import functools
import math

import jax
import jax.numpy as jnp
from jax import lax
from jax.experimental import pallas as pl
from jax.experimental.pallas import tpu as pltpu

F32 = jnp.float32
BF16 = jnp.bfloat16

VMEM_LIMIT_BYTES = 56 * 1024 * 1024
LANES = 128
SUBLANES = 8

N_MIXERS = 3
LN_EPS = 1e-5
GRID_W = 64
CONV_W = 4
RG_C = 8.0
RG_BLOCKS = 8
RW_HEAD = 64
RW_DECAY_SCALE = math.exp(-0.5)
RW_GN_EPS = 64e-5
RET_HEADS = 8
RET_CHUNK = 128
N_KEYS = 128
PEER_HEADS = 8
PEER_TOPK = 16
ROW_TILE = 256


def _pick(n, candidates):
    for t in candidates:
        if n % t == 0:
            return t
    return n


def _params(*sem):
    return pltpu.CompilerParams(dimension_semantics=sem, vmem_limit_bytes=VMEM_LIMIT_BYTES)


def _mm_kernel(a_ref, w_ref, o_ref, *, act):
    acc = jnp.dot(a_ref[...], w_ref[...], preferred_element_type=F32)
    if act == "tanh":
        acc = jnp.tanh(acc)
    elif act == "sigmoid":
        acc = jax.nn.sigmoid(acc)
    o_ref[...] = acc.astype(o_ref.dtype)


def matmul(a, w, out_dtype=F32, act=None):
    M, K = a.shape
    N = w.shape[1]
    tm = _pick(M, (512, 256, 128))
    tn = _pick(N, (512, 256, 128))
    return pl.pallas_call(
        functools.partial(_mm_kernel, act=act),
        grid=(M // tm, N // tn),
        in_specs=[pl.BlockSpec((tm, K), lambda i, j: (i, 0)),
                  pl.BlockSpec((K, tn), lambda i, j: (0, j))],
        out_specs=pl.BlockSpec((tm, tn), lambda i, j: (i, j)),
        out_shape=jax.ShapeDtypeStruct((M, N), out_dtype),
        compiler_params=_params("parallel", "parallel"),
    )(a, w)


def _ln_mod_kernel(x_ref, y_ref, mg_ref, mm_ref, g_ref, b_ref, xo_ref, ho_ref, *, alpha, gate, shift, scale):
    z = alpha * x_ref[...] + mg_ref[gate:gate + 1, :] * y_ref[...]
    mu = jnp.mean(z, axis=-1, keepdims=True)
    zc = z - mu
    var = jnp.mean(zc * zc, axis=-1, keepdims=True)
    xn = zc * lax.rsqrt(var + LN_EPS) * g_ref[...] + b_ref[...]
    xo_ref[...] = xn
    ho_ref[...] = (xn * (1.0 + mm_ref[scale:scale + 1, :]) + mm_ref[shift:shift + 1, :]).astype(ho_ref.dtype)


def ln_mod(x, y, mod_gate, gate, mod_next, shift, scale, g, b, alpha, ctx_tiles, tiles_per_row, h_dtype):
    T, D = x.shape

    def mod_map(i):
        return (i // tiles_per_row, jnp.where((i % tiles_per_row) >= ctx_tiles, 1, 0), 0, 0)

    return pl.pallas_call(
        functools.partial(_ln_mod_kernel, alpha=alpha, gate=gate, shift=shift, scale=scale),
        grid=(T // ROW_TILE,),
        in_specs=[pl.BlockSpec((ROW_TILE, D), lambda i: (i, 0)),
                  pl.BlockSpec((ROW_TILE, D), lambda i: (i, 0)),
                  pl.BlockSpec((None, None, 6, D), mod_map),
                  pl.BlockSpec((None, None, 6, D), mod_map),
                  pl.BlockSpec((1, D), lambda i: (0, 0)),
                  pl.BlockSpec((1, D), lambda i: (0, 0))],
        out_specs=[pl.BlockSpec((ROW_TILE, D), lambda i: (i, 0)),
                   pl.BlockSpec((ROW_TILE, D), lambda i: (i, 0))],
        out_shape=[jax.ShapeDtypeStruct((T, D), F32), jax.ShapeDtypeStruct((T, D), h_dtype)],
        compiler_params=_params("parallel"),
    )(x, y, mod_gate, mod_next, g.reshape(1, D), b.reshape(1, D))


def _scan_chunk(d, c, ctx_chunks, n_chunks):
    rev = jnp.where(c < ctx_chunks, ctx_chunks - 1 - c, n_chunks - 1 - (c - ctx_chunks))
    return jnp.where(d == 0, c, rev)


def _softplus(x):
    return jnp.maximum(x, 0.0) + jnp.log1p(jnp.exp(-jnp.abs(x)))


def _rg_coef_kernel(xc_ref, gw_ref, gb_ref, lam_ref, a_ref, b_ref):
    xc = xc_ref[...]
    xcb = xc.astype(BF16)
    bs = xc.shape[1] // RG_BLOCKS
    for d in range(2):
        for n in range(RG_BLOCKS):
            sl = slice(n * bs, (n + 1) * bs)
            xs = xcb[:, sl]
            r_gate = jax.nn.sigmoid(jnp.dot(xs, gw_ref[d, 0, n], preferred_element_type=F32) + gb_ref[d, 0:1, sl])
            i_gate = jax.nn.sigmoid(jnp.dot(xs, gw_ref[d, 1, n], preferred_element_type=F32) + gb_ref[d, 1:2, sl])
            log_a = (-RG_C) * r_gate * _softplus(-lam_ref[d:d + 1, sl])
            a = jnp.exp(log_a)
            a_ref[d, :, sl] = a
            b_ref[d, :, sl] = jnp.sqrt(-jnp.tanh(log_a) * (a * a + 1.0)) * (i_gate * xc[:, sl])


def rg_coef(xc, gate_w, gate_b, lam):
    T, C = xc.shape
    bs = C // RG_BLOCKS
    return pl.pallas_call(
        _rg_coef_kernel,
        grid=(T // ROW_TILE,),
        in_specs=[pl.BlockSpec((ROW_TILE, C), lambda i: (i, 0)),
                  pl.BlockSpec((2, 2, RG_BLOCKS, bs, bs), lambda i: (0, 0, 0, 0, 0)),
                  pl.BlockSpec((2, 2, C), lambda i: (0, 0, 0)),
                  pl.BlockSpec((2, C), lambda i: (0, 0))],
        out_specs=[pl.BlockSpec((2, ROW_TILE, C), lambda i: (0, i, 0)),
                   pl.BlockSpec((2, ROW_TILE, C), lambda i: (0, i, 0))],
        out_shape=[jax.ShapeDtypeStruct((2, T, C), F32), jax.ShapeDtypeStruct((2, T, C), F32)],
        compiler_params=_params("parallel"),
    )(xc, gate_w.astype(BF16), gate_b, lam)


def _lin_scan_kernel(a_ref, b_ref, o_ref, h_ref, *, tc):
    d = pl.program_id(0)

    @pl.when(pl.program_id(2) == 0)
    def _():
        h_ref[...] = jnp.zeros_like(h_ref)

    def run(rev):
        def body(i, h):
            t = (tc - 1 - i) if rev else i
            h = a_ref[t] * h + b_ref[t]
            o_ref[t] = h
            return h
        h_ref[...] = lax.fori_loop(0, tc, body, h_ref[...], unroll=8)

    @pl.when(d == 0)
    def _():
        run(False)

    @pl.when(d == 1)
    def _():
        run(True)


def lin_scan(a, b, ctx_len):
    _, B, S, C = a.shape
    tc = ROW_TILE
    cs = C // LANES
    a5 = a.reshape(2, B, S, cs, LANES)
    b5 = b.reshape(2, B, S, cs, LANES)
    n_chunks, ctx_chunks = S // tc, ctx_len // tc
    imap = lambda d, bb, c: (d, bb, _scan_chunk(d, c, ctx_chunks, n_chunks), 0, 0)
    spec = pl.BlockSpec((None, None, tc, cs, LANES), imap)
    out = pl.pallas_call(
        functools.partial(_lin_scan_kernel, tc=tc),
        grid=(2, B, n_chunks),
        in_specs=[spec, spec],
        out_specs=spec,
        out_shape=jax.ShapeDtypeStruct(a5.shape, F32),
        scratch_shapes=[pltpu.VMEM((cs, LANES), F32)],
        compiler_params=_params("arbitrary", "arbitrary", "arbitrary"),
    )(a5, b5)
    return out.reshape(2, B, S, C)


def _rwkv_scan_kernel(r_ref, kk_ref, w_ref, ka_ref, kd_ref, v_ref, y_ref, s_ref, *, tc):
    nk = s_ref.shape[0]
    d = pl.program_id(0)

    @pl.when(pl.program_id(1) == 0)
    def _():
        s_ref[...] = jnp.zeros_like(s_ref)

    def step(i, carry):
        t = i + d * (tc - 1 - 2 * i)
        parts = [None] * 4
        for k in range(nk):
            term = s_ref[k] * kk_ref[t, k:k + 1, :]
            parts[k % 4] = term if parts[k % 4] is None else parts[k % 4] + term
        sa = (parts[0] + parts[1]) + (parts[2] + parts[3])
        v = v_ref[t]
        yp = [None] * 4
        for k in range(nk):
            sk = s_ref[k] * w_ref[t, k:k + 1, :] - sa * ka_ref[t, k:k + 1, :] + v * kd_ref[t, k:k + 1, :]
            s_ref[k] = sk
            term = sk * r_ref[t, k:k + 1, :]
            yp[k % 4] = term if yp[k % 4] is None else yp[k % 4] + term
        y_ref[t] = (yp[0] + yp[1]) + (yp[2] + yp[3])
        return carry

    lax.fori_loop(0, tc, step, 0)


def rwkv_scan(r, kk, w, ka, kd, v, ctx_len):
    S, N, BH = r.shape
    tc = 32
    n_chunks, ctx_chunks = S // tc, ctx_len // tc
    chunk = lambda d, c: _scan_chunk(d, c, ctx_chunks, n_chunks)
    shared = pl.BlockSpec((tc, N, BH), lambda d, c: (chunk(d, c), 0, 0))
    per_dir = pl.BlockSpec((None, tc, N, BH), lambda d, c: (d, chunk(d, c), 0, 0))
    return pl.pallas_call(
        functools.partial(_rwkv_scan_kernel, tc=tc),
        grid=(2, n_chunks),
        in_specs=[shared, shared, per_dir, per_dir, per_dir, shared],
        out_specs=per_dir,
        out_shape=jax.ShapeDtypeStruct((2, S, N, BH), F32),
        scratch_shapes=[pltpu.VMEM((N, N, BH), F32)],
        compiler_params=_params("arbitrary", "arbitrary"),
    )(r, kk, w, ka, kd, v)


def _ret_kernel(q_ref, k_ref, v_ref, mask_ref, xi_ref, zeta_ref, cd_ref, o_ref, r_ref):
    @pl.when(pl.program_id(3) == 0)
    def _():
        r_ref[...] = jnp.zeros_like(r_ref)

    q = q_ref[...]
    k = k_ref[...]
    vb = v_ref[...].astype(BF16)
    s = lax.dot_general(q.astype(BF16), k.astype(BF16), (((1,), (1,)), ((), ())),
                        preferred_element_type=F32) * mask_ref[...]
    o = jnp.dot(s.astype(BF16), vb, preferred_element_type=F32)
    o = o + jnp.dot((q * xi_ref[...]).astype(BF16), r_ref[...].astype(BF16), preferred_element_type=F32)
    kz = (k * zeta_ref[...]).astype(BF16)
    r_ref[...] = r_ref[...] * cd_ref[...] + lax.dot_general(kz, vb, (((0,), (0,)), ((), ())),
                                                             preferred_element_type=F32)
    o_ref[...] = o


def retention(q, k, v, ctx_len):
    B, S, _ = q.shape
    H = RET_HEADS
    dk, dv = q.shape[2] // H, v.shape[2] // H
    C = RET_CHUNK
    log_gamma = jnp.log1p(-jnp.exp2(-5.0 - jnp.arange(H, dtype=F32)))[:, None, None]
    pos = jnp.arange(C, dtype=F32)
    diff = pos[:, None] - pos[None, :]
    mask_f = jnp.where(diff >= 0, jnp.exp(jnp.maximum(diff, 0.0) * log_gamma), 0.0)
    mask = jnp.stack([mask_f, jnp.swapaxes(mask_f, 1, 2)], axis=1)
    xi_f = jnp.exp((pos + 1.0) * log_gamma[:, 0])
    xi = jnp.stack([xi_f, xi_f[:, ::-1]], axis=1)[..., None] * jnp.ones((dk,), F32)
    zeta_f = jnp.exp((C - 1.0 - pos) * log_gamma[:, 0])
    zeta = jnp.stack([zeta_f, zeta_f[:, ::-1]], axis=1)[..., None] * jnp.ones((dk,), F32)
    cd = jnp.exp(C * log_gamma) * jnp.ones((1, dv), F32)
    n_chunks, ctx_chunks = S // C, ctx_len // C
    chunk = lambda d, c: _scan_chunk(d, c, ctx_chunks, n_chunks)
    return pl.pallas_call(
        _ret_kernel,
        grid=(B, H, 2, n_chunks),
        in_specs=[pl.BlockSpec((None, C, dk), lambda b, h, d, c: (b, chunk(d, c), h)),
                  pl.BlockSpec((None, C, dk), lambda b, h, d, c: (b, chunk(d, c), h)),
                  pl.BlockSpec((None, C, dv), lambda b, h, d, c: (b, chunk(d, c), h)),
                  pl.BlockSpec((None, None, C, C), lambda b, h, d, c: (h, d, 0, 0)),
                  pl.BlockSpec((None, None, C, dk), lambda b, h, d, c: (h, d, 0, 0)),
                  pl.BlockSpec((None, None, C, dk), lambda b, h, d, c: (h, d, 0, 0)),
                  pl.BlockSpec((None, 1, dv), lambda b, h, d, c: (h, 0, 0))],
        out_specs=pl.BlockSpec((None, None, C, dv), lambda b, h, d, c: (d, b, chunk(d, c), h)),
        out_shape=jax.ShapeDtypeStruct((2, B, S, H * dv), F32),
        scratch_shapes=[pltpu.VMEM((dk, dv), F32)],
        compiler_params=_params("arbitrary", "arbitrary", "arbitrary", "arbitrary"),
    )(q, k, v, mask, xi, zeta, cd)


def _cmpx(v, i, j):
    hi, lo = jnp.maximum(v[i], v[j]), jnp.minimum(v[i], v[j])
    v[i], v[j] = hi, lo


def _bitonic_sort_desc(v):
    n = len(v)
    k = 2
    while k <= n:
        j = k // 2
        while j >= 1:
            for i in range(n):
                l = i ^ j
                if l > i:
                    if (i & k) == 0:
                        _cmpx(v, i, l)
                    else:
                        _cmpx(v, l, i)
            j //= 2
        k *= 2
    return v


def _bitonic_merge_desc(v):
    j = len(v) // 2
    while j >= 1:
        for i in range(len(v)):
            l = i ^ j
            if l > i:
                _cmpx(v, i, l)
        j //= 2
    return v


def _merge_top(a, b):
    n = len(a)
    return _bitonic_merge_desc([jnp.maximum(a[r], b[n - 1 - r]) for r in range(n)])


def _top16_rows(s):
    groups = [s[SUBLANES * r:SUBLANES * (r + 1), :] for r in range(s.shape[0] // SUBLANES)]
    top = _bitonic_sort_desc(groups)
    for sh in (4, 2, 1):
        top = _merge_top(top, [pltpu.roll(t, sh, 0) for t in top])
    return top


def _peer_score_kernel(q_ref, keys_ref, s1_ref, c_ref, s2_ref, e2_ref, tau_ref):
    tm = q_ref.shape[0]
    dh = keys_ref.shape[3]
    sub = lax.broadcasted_iota(jnp.int32, (SUBLANES, tm), 0)
    a_pack = [jnp.zeros((SUBLANES, tm), F32) for _ in range(PEER_TOPK)]
    b_pack = [jnp.zeros((SUBLANES, tm), F32) for _ in range(PEER_TOPK)]
    for h in range(PEER_HEADS):
        for p in range(2):
            col = (2 * h + p) * dh
            s = lax.dot_general(keys_ref[h, p], q_ref[:, col:col + dh], (((1,), (1,)), ((), ())),
                                preferred_element_type=F32)
            (s1_ref if p == 0 else s2_ref)[h] = s
            top = _top16_rows(s)
            pack = a_pack if p == 0 else b_pack
            for r in range(PEER_TOPK):
                pack[r] = jnp.where(sub == h, top[r], pack[r])
    first = [a_pack[0] + b_pack[j] for j in range(PEER_TOPK)]
    rest = [a_pack[i] + b_pack[j] for i in range(1, PEER_TOPK) for j in range(PEER_TOPK)
            if (i + 1) * (j + 1) <= PEER_TOPK]
    neg = jnp.full((SUBLANES, tm), -jnp.inf, F32)
    rest = rest + [neg] * (-len(rest) % PEER_TOPK)
    best = first
    for g in range(len(rest) // PEER_TOPK):
        best = _merge_top(best, _bitonic_sort_desc(rest[g * PEER_TOPK:(g + 1) * PEER_TOPK]))
    tau = best[PEER_TOPK - 1]
    m = best[0]
    z = jnp.exp(best[0] - m)
    for r in range(1, PEER_TOPK):
        z = z + jnp.exp(best[r] - m)
    tau_ref[...] = tau
    inv_z = 1.0 / z
    for h in range(PEER_HEADS):
        c_ref[h] = jnp.exp(s1_ref[h] - a_pack[0][h:h + 1, :]) * inv_z[h:h + 1, :]
        e2_ref[h] = jnp.exp(s2_ref[h] - b_pack[0][h:h + 1, :])


def peer_scores(q, keys):
    T = q.shape[0]
    H, _, nk, dh = keys.shape
    tm = ROW_TILE
    tab = jax.ShapeDtypeStruct((H, nk, T), F32)
    tab_spec = pl.BlockSpec((H, nk, tm), lambda i: (0, 0, i))
    return pl.pallas_call(
        _peer_score_kernel,
        grid=(T // tm,),
        in_specs=[pl.BlockSpec((tm, q.shape[1]), lambda i: (i, 0)),
                  pl.BlockSpec((H, 2, nk, dh), lambda i: (0, 0, 0, 0))],
        out_specs=[tab_spec, tab_spec, tab_spec, tab_spec, pl.BlockSpec((H, tm), lambda i: (0, i))],
        out_shape=[tab, tab, tab, tab, jax.ShapeDtypeStruct((H, T), F32)],
        compiler_params=_params("parallel"),
    )(q, keys)


def _gelu_tanh(x):
    return 0.5 * x * (1.0 + jnp.tanh(0.7978845608028654 * (x + 0.044715 * (x * x * x))))


def _peer_dense_kernel(ht_ref, u_ref, v_ref, s1_ref, c_ref, s2_ref, e2_ref, tau_ref, o_ref, act_scr, p_scr, *, te):
    e = pl.program_id(1)
    et = jnp.maximum(e - 1, 0)
    tm = ht_ref.shape[1]
    th = tm // 2
    rows = te // N_KEYS
    group = pl.multiple_of((et * rows) // SUBLANES * SUBLANES, SUBLANES)
    offset = (et * rows) % SUBLANES
    cur = e % 2

    @pl.when(e == 0)
    def _():
        o_ref[...] = jnp.zeros_like(o_ref)
        act_scr[0] = jnp.zeros((te, tm), F32)

    def table_row(ref, h, ls, ii):
        blk = ref[h, pl.ds(group, SUBLANES), ls]
        row = blk[ii:ii + 1, :]
        for o in range(rows, SUBLANES, rows):
            row = jnp.where(offset == o, blk[o + ii:o + ii + 1, :], row)
        return row

    for half in range(2):
        for ii in range(rows):
            for lb in range(th // LANES):
                l0 = half * th + lb * LANES
                ls = slice(l0, l0 + LANES)
                w = None
                for h in range(PEER_HEADS):
                    ssum = s2_ref[h, :, ls] + table_row(s1_ref, h, ls, ii)
                    g = jnp.where(ssum >= tau_ref[h:h + 1, ls], e2_ref[h, :, ls] * table_row(c_ref, h, ls, ii), 0.0)
                    w = g if w is None else w + g
                p_scr[ii * N_KEYS:(ii + 1) * N_KEYS, ls] = w * act_scr[cur, ii * N_KEYS:(ii + 1) * N_KEYS, ls]
        hs = slice(half * th, (half + 1) * th)
        p = p_scr[:, hs].T.astype(BF16)
        o_ref[hs, :] += jnp.dot(p, v_ref[...], preferred_element_type=F32)

    act_scr[1 - cur] = _gelu_tanh(jnp.dot(u_ref[...], ht_ref[...], preferred_element_type=F32))


def peer_dense(ht, u, v, s1, c, s2, e2, tau):
    D, T = ht.shape
    E = u.shape[0]
    H = s1.shape[0]
    tm = _pick(T, (512, 256))
    te = 512
    n_et = E // te
    tab_spec = pl.BlockSpec((H, N_KEYS, tm), lambda i, e: (0, 0, i))
    return pl.pallas_call(
        functools.partial(_peer_dense_kernel, te=te),
        grid=(T // tm, n_et + 1),
        in_specs=[pl.BlockSpec((D, tm), lambda i, e: (0, i)),
                  pl.BlockSpec((te, D), lambda i, e: (jnp.minimum(e, n_et - 1), 0)),
                  pl.BlockSpec((te, D), lambda i, e: (jnp.maximum(e - 1, 0), 0)),
                  tab_spec, tab_spec, tab_spec, tab_spec,
                  pl.BlockSpec((H, tm), lambda i, e: (0, i))],
        out_specs=pl.BlockSpec((tm, D), lambda i, e: (i, 0)),
        out_shape=jax.ShapeDtypeStruct((T, D), F32),
        scratch_shapes=[pltpu.VMEM((2, te, tm), F32), pltpu.VMEM((te, tm), F32)],
        compiler_params=_params("parallel", "arbitrary"),
    )(ht, u, v, s1, c, s2, e2, tau)


def peer_ffn(h, w_q, keys, u_tab, v_tab):
    q = matmul(h, w_q.astype(BF16), out_dtype=BF16)
    s1, c, s2, e2, tau = peer_scores(q, keys.astype(BF16))
    return peer_dense(h.T, u_tab.astype(BF16), v_tab.astype(BF16), s1, c, s2, e2, tau)


def _head_norm(y, eps):
    mu = jnp.mean(y, -1, keepdims=True)
    var = jnp.mean(jnp.square(y - mu), -1, keepdims=True)
    return (y - mu) * lax.rsqrt(var + eps)


def _conv_centred(x, w, b):
    L = x.shape[1]
    xp = jnp.pad(x, ((0, 0), (CONV_W // 2, CONV_W - 1 - CONV_W // 2), (0, 0)))
    return sum(xp[:, j:j + L] * w[j] for j in range(CONV_W)) + b


def rglru_mixer(h, ctx_len, w_in, conv_w, conv_b, gate_w, gate_b, lam, w_out):
    B, S, D = h.shape
    T = B * S
    C = w_in.shape[1] // 2
    u = matmul(h.reshape(T, D), w_in.astype(BF16))
    ur = u[:, C:].reshape(B, S, C)
    xc = jnp.concatenate([_conv_centred(ur[:, :ctx_len], conv_w, conv_b),
                          _conv_centred(ur[:, ctx_len:], conv_w, conv_b)], axis=1)
    a, b = rg_coef(xc.reshape(T, C), gate_w, gate_b, lam)
    rec = lin_scan(a.reshape(2, B, S, C), b.reshape(2, B, S, C), ctx_len)
    z = jax.nn.gelu(u[:, :C]) * (rec[0] + rec[1]).reshape(T, C)
    return matmul(z.astype(BF16), w_out.astype(BF16))


def _q_shift(x):
    Bn, L, D = x.shape
    g = x.reshape(Bn, L // GRID_W, GRID_W, D)
    q = D // 4
    left = jnp.pad(g[:, :, :-1, :q], ((0, 0), (0, 0), (1, 0), (0, 0)))
    right = jnp.pad(g[:, :, 1:, q:2 * q], ((0, 0), (0, 0), (0, 1), (0, 0)))
    up = jnp.pad(g[:, :-1, :, 2 * q:3 * q], ((0, 0), (1, 0), (0, 0), (0, 0)))
    down = jnp.pad(g[:, 1:, :, 3 * q:], ((0, 0), (0, 1), (0, 0), (0, 0)))
    return jnp.concatenate([left, right, up, down], -1).reshape(Bn, L, D)


def _seq_shift(x):
    hh = x.shape[-1] // 2
    prev = jnp.pad(x[:, :-1, :hh], ((0, 0), (1, 0), (0, 0)))
    nxt = jnp.pad(x[:, 1:, hh:], ((0, 0), (0, 1), (0, 0)))
    return jnp.concatenate([prev, nxt], -1)


def _pad_cols(w):
    return jnp.pad(w, ((0, 0), (0, -w.shape[1] % LANES)))


def _pad_rows(w):
    return jnp.pad(w, ((0, -w.shape[0] % LANES), (0, 0)))


def rwkv7_mixer(h, ctx_len, mu, w_rkv, w_o, dec0, dec1, dec2, icl0, icl1, icl2, g1, g2, k_k, k_a, r_k, gn_g, gn_b):
    B, S, D = h.shape
    T = B * S
    H, N = D // RW_HEAD, RW_HEAD
    shifted = jnp.concatenate([_seq_shift(h[:, :ctx_len]), _q_shift(h[:, ctx_len:])], axis=1)
    xx = shifted - h
    mix = lambda m: (h + xx * mu[m]).reshape(T, D).astype(BF16)
    r = matmul(mix(0), w_rkv[0].astype(BF16))
    k = matmul(mix(1), w_rkv[1].astype(BF16))
    v = matmul(mix(2), w_rkv[2].astype(BF16))
    xw, xa = mix(3), mix(4)
    kn = (k * k_k).reshape(T, H, N)
    kk = (kn * lax.rsqrt(jnp.sum(kn * kn, -1, keepdims=True) + 1e-12)).reshape(T, D)

    def reorder(t):
        return t.reshape(B, S, H, N).transpose(1, 3, 0, 2).reshape(S, N, B * H)

    def restore(t):
        return t.reshape(S, N, B, H).transpose(2, 0, 3, 1).reshape(T, D)

    ws, kas, kds, bonus = [], [], [], 0.0
    for d in range(2):
        lw = matmul(matmul(xw, _pad_cols(dec1[d]).astype(BF16), out_dtype=BF16, act="tanh"),
                    _pad_rows(dec2[d]).astype(BF16))
        w = jnp.exp(-RW_DECAY_SCALE * jax.nn.sigmoid(dec0[d] + lw))
        la = matmul(matmul(xa, _pad_cols(icl1[d]).astype(BF16), out_dtype=BF16), _pad_rows(icl2[d]).astype(BF16))
        a = jax.nn.sigmoid(icl0[d] + la)
        kd = k * (1.0 + (a - 1.0) * k_a)
        ws.append(reorder(w))
        kas.append(reorder(kk * a))
        kds.append(reorder(kd))
        bonus = bonus + jnp.sum((r * kd).reshape(T, H, N) * r_k, -1, keepdims=True) * v.reshape(T, H, N)
    ys = rwkv_scan(reorder(r), reorder(kk), jnp.stack(ws), jnp.stack(kas), jnp.stack(kds), reorder(v), ctx_len)
    ysum = restore(ys[0] + ys[1]).reshape(T, H, N)
    y = _head_norm(ysum, RW_GN_EPS).reshape(T, D) * gn_g + gn_b
    g = matmul(matmul(mix(5), g1.astype(BF16), out_dtype=BF16, act="sigmoid"), g2.astype(BF16))
    return matmul(((y + bonus.reshape(T, D)) * g).astype(BF16), w_o.astype(BF16))


def _rotary(t, pos):
    half = t.shape[-1] // 2
    theta = 1.0 / (10000.0 ** jnp.linspace(0.0, 1.0, half, dtype=F32))
    ang = pos[:, None] * theta[None, :]
    cos = jnp.cos(ang)[None, :, None, :]
    sin = jnp.sin(ang)[None, :, None, :]
    t1, t2 = t[..., :half], t[..., half:]
    return jnp.concatenate([t1 * cos - t2 * sin, t1 * sin + t2 * cos], -1)


def retention_mixer(h, ctx_len, w_in, w_out):
    B, S, D = h.shape
    T = B * S
    H = RET_HEADS
    qk = D
    dv = 2 * (D // H)
    nv = H * dv
    u = matmul(h.reshape(T, D), w_in.astype(BF16)).reshape(B, S, -1)
    pos = jnp.arange(S, dtype=F32)
    q = _rotary(u[..., :qk].reshape(B, S, H, -1), pos).reshape(B, S, qk)
    k = (_rotary(u[..., qk:2 * qk].reshape(B, S, H, -1), pos) * (qk // H) ** -0.5).reshape(B, S, qk)
    v = u[..., 2 * qk:2 * qk + nv]
    o = retention(q, k, v, ctx_len)
    g_f = u[..., 2 * qk + nv:2 * qk + 2 * nv]
    g_b = u[..., 2 * qk + 2 * nv:]
    n_f = _head_norm(o[0].reshape(B, S, H, dv), LN_EPS).reshape(B, S, nv)
    n_b = _head_norm(o[1].reshape(B, S, H, dv), LN_EPS).reshape(B, S, nv)
    z = jax.nn.silu(g_f) * n_f + jax.nn.silu(g_b) * n_b
    return matmul(z.reshape(T, nv).astype(BF16), w_out.astype(BF16))


def kernel(x, c, ctx, c_ctx, ada_w, ada_b, ln_g, ln_b, peer_wq, peer_keys, peer_u, peer_v, rg_w_in, rg_conv_w, rg_conv_b, rg_gate_w, rg_gate_b, rg_lam, rg_w_out, rw_mu, rw_w_rkv, rw_w_o, rw_dec0, rw_dec1, rw_dec2, rw_icl0, rw_icl1, rw_icl2, rw_g1, rw_g2, rw_k_k, rw_k_a, rw_r_k, rw_gn_g, rw_gn_b, ret_w_in, ret_w_out):
    B, L, D = x.shape
    ctx_len = ctx.shape[1]
    S = ctx_len + L
    T = B * S
    depth = ada_w.shape[0]
    alpha = (2 * depth) ** 0.25
    assert ctx_len % ROW_TILE == 0 and L % ROW_TILE == 0

    cond = jnp.zeros((16, D), F32).at[:B].set(c).at[B].set(c_ctx)
    sc = jax.nn.silu(cond).astype(BF16)
    mods = []
    for i in range(depth):
        m = (matmul(sc, ada_w[i].astype(BF16)) + ada_b[i]).reshape(16, 6, D)
        mods.append(jnp.stack([jnp.broadcast_to(m[B], (B, 6, D)), m[:B]], axis=1))

    tiles_per_row, ctx_tiles = S // ROW_TILE, ctx_len // ROW_TILE
    is_lat = (jnp.arange(S) >= ctx_len)[None, :, None]
    xs = jnp.concatenate([ctx, x], axis=1)
    rowmod = lambda m, j: jnp.where(is_lat, m[:, 1, j][:, None, :], m[:, 0, j][:, None, :])
    mixer_dtype = lambda i: F32 if i % N_MIXERS == 1 else BF16
    h = (xs * (1.0 + rowmod(mods[0], 1)) + rowmod(mods[0], 0)).astype(mixer_dtype(0))
    xs = xs.reshape(T, D)

    for i in range(depth):
        kind, j = i % N_MIXERS, i // N_MIXERS
        if kind == 0:
            y = rglru_mixer(h, ctx_len, rg_w_in[j], rg_conv_w[j], rg_conv_b[j], rg_gate_w[j], rg_gate_b[j],
                            rg_lam[j], rg_w_out[j])
        elif kind == 1:
            y = rwkv7_mixer(h, ctx_len, rw_mu[j], rw_w_rkv[j], rw_w_o[j], rw_dec0[j], rw_dec1[j], rw_dec2[j],
                            rw_icl0[j], rw_icl1[j], rw_icl2[j], rw_g1[j], rw_g2[j], rw_k_k[j], rw_k_a[j],
                            rw_r_k[j], rw_gn_g[j], rw_gn_b[j])
        else:
            y = retention_mixer(h, ctx_len, ret_w_in[j], ret_w_out[j])
        xs, h2 = ln_mod(xs, y, mods[i], 2, mods[i], 3, 4, ln_g[i, 0], ln_b[i, 0], alpha, ctx_tiles,
                        tiles_per_row, BF16)
        y2 = peer_ffn(h2, peer_wq[i], peer_keys[i], peer_u[i], peer_v[i])
        nxt = mods[min(i + 1, depth - 1)]
        xs, h = ln_mod(xs, y2, mods[i], 5, nxt, 0, 1, ln_g[i, 1], ln_b[i, 1], alpha, ctx_tiles,
                       tiles_per_row, mixer_dtype(i + 1))
        h = h.reshape(B, S, D)
    return xs.reshape(B, S, D)[:, ctx_len:]
```
